```python
import math
import jax, jax.numpy as jnp
from jax import lax
import numpy as np

D_MODEL = 1024
BATCH = 4
SEQ = 4096
DEPTH = 1
DEC_BATCH = 128
DEC_SEQ = 1
PAST_LEN = 8192
PAGE_SIZE = 128

N_META = 16
MIX_WIDTH = D_MODEL
EPS = 1e-6
HG_WIDTH = MIX_WIDTH // 2
HG_HEADS = 4
HG_DV = HG_WIDTH // HG_HEADS
HG_DK = 128
HG_QK = HG_HEADS * HG_DK
CHUNK = 128
SUB = 16
NS = CHUNK // SUB
SW_WIDTH = MIX_WIDTH - HG_WIDTH
SW_HEAD_DIM = 64
SW_HEADS = SW_WIDTH // SW_HEAD_DIM
SW_KV_HEADS = 2
SW_GROUP = SW_HEADS // SW_KV_HEADS
WINDOW = 128
SW_SCALE = SW_HEAD_DIM ** -0.5
ROPE_THETA = 10000.0
IN_SPLITS = (HG_QK, HG_QK, HG_WIDTH, HG_WIDTH,
             SW_HEADS * SW_HEAD_DIM, SW_KV_HEADS * SW_HEAD_DIM, SW_KV_HEADS * SW_HEAD_DIM)
IN_WIDTH = sum(IN_SPLITS)
PEER_HEADS = 8
N_KEYS = 128
N_EXPERTS = N_KEYS * N_KEYS
PEER_TOPK = 16
D_KEY = 128
PEER_BLOCK = 128

kernel_name = "hymba_hgrn2_swa_sink_peer_step"

F32 = jnp.float32


def rmsnorm(x, g):
    xf = x.astype(F32)
    y = xf * lax.rsqrt(jnp.mean(xf * xf, axis=-1, keepdims=True) + EPS)
    return (y * g.astype(F32)).astype(x.dtype)


def rotary(x, pos):
    half = x.shape[-1] // 2
    inv = ROPE_THETA ** (-jnp.arange(half, dtype=F32) * 2.0 / x.shape[-1])
    ang = pos.astype(F32)[:, None] * inv[None, :]
    cos = jnp.cos(ang)[:, None, :]
    sin = jnp.sin(ang)[:, None, :]
    xf = x.astype(F32)
    x1, x2 = xf[..., :half], xf[..., half:]
    return jnp.concatenate([x1 * cos - x2 * sin, x2 * cos + x1 * sin], axis=-1).astype(x.dtype)


def split_proj(p):
    out = []
    start = 0
    for size in IN_SPLITS:
        out.append(p[..., start:start + size])
        start += size
    return out


def hgrn_features(q_raw, f_raw, i_raw, lb):
    b_, s_ = q_raw.shape[:2]
    q = jax.nn.silu(q_raw.astype(F32)).reshape(b_, s_, HG_HEADS, HG_DK)
    f = (lb + (1.0 - lb) * jax.nn.sigmoid(f_raw.astype(F32))).reshape(b_, s_, HG_HEADS, HG_DK)
    k = 1.0 - f
    v = i_raw.astype(F32).reshape(b_, s_, HG_HEADS, HG_DV)
    return q, f, k, v


def hgrn_chunk_step(s, inp):
    q, g, k, v = inp
    lead = q.shape[:2]
    b = jnp.cumsum(g, axis=-2)
    b_last = b[..., -1:, :]
    o = jnp.einsum('bhck,bhkv->bhcv', q * jnp.exp(b), s)
    qs = q.reshape(*lead, NS, SUB, HG_DK)
    ks = k.reshape(*lead, NS, SUB, HG_DK)
    vs = v.reshape(*lead, NS, SUB, HG_DV)
    bs = b.reshape(*lead, NS, SUB, HG_DK)
    b_end = bs[..., -1, :]
    b_start = jnp.concatenate([jnp.zeros_like(b_end[..., :1, :]), b_end[..., :-1, :]], axis=-2)
    q_in = qs * jnp.exp(bs - b_start[..., None, :])
    k_in = ks * jnp.exp(b_end[..., None, :] - bs)
    sub_ids = jnp.arange(NS)
    earlier = sub_ids[None, :] < sub_ids[:, None]
    gap = b_start[..., :, None, :] - b_end[..., None, :, :]
    decay = jnp.exp(jnp.where(earlier[:, :, None], gap, -jnp.inf))
    a_cross = jnp.einsum('bhitk,bhijk,bhjsk->bhitjs', q_in, decay, k_in)
    pos = jnp.arange(SUB)
    causal = pos[None, :] <= pos[:, None]
    rel = bs[..., :, None, :] - bs[..., None, :, :]
    rel_decay = jnp.exp(jnp.where(causal[:, :, None], rel, -jnp.inf))
    a_diag = jnp.einsum('bhitk,bhitsk,bhisk->bhits', qs, rel_decay, ks)
    o_intra = (jnp.einsum('bhitjs,bhjsv->bhitv', a_cross, vs)
               + jnp.einsum('bhits,bhisv->bhitv', a_diag, vs))
    o = o + o_intra.reshape(*lead, CHUNK, HG_DV)
    s_new = (jnp.exp(b_last[..., 0, :])[..., None] * s
             + jnp.einsum('bhck,bhcv->bhkv', k * jnp.exp(b_last - b), v))
    return s_new, o


def hgrn_chunked(q, logf, k, v):
    b_, lp = q.shape[:2]
    nc = lp // CHUNK

    def to_chunks(t):
        return t.reshape(b_, nc, CHUNK, HG_HEADS, t.shape[-1]).transpose(1, 0, 3, 2, 4)

    xs = (to_chunks(q), to_chunks(logf), to_chunks(k), to_chunks(v))
    s0 = jnp.zeros((b_, HG_HEADS, HG_DK, HG_DV), F32)
    s_fin, o = lax.scan(hgrn_chunk_step, s0, xs)
    o = o.transpose(1, 0, 3, 2, 4).reshape(b_, lp, HG_HEADS, HG_DV)
    return o, s_fin


def hgrn_token_step(s, inp):
    q, f, k, v = inp
    s = f[..., None] * s + k[..., None] * v[..., None, :]
    return s, jnp.einsum('bhk,bhkv->bhv', q, s)


def hgrn_readout(o, gate, gnorm):
    b_, s_ = o.shape[:2]
    o = o * lax.rsqrt(jnp.mean(o * o, axis=-1, keepdims=True) + EPS) * gnorm.astype(F32)
    return o.reshape(b_, s_, HG_WIDTH) * jax.nn.silu(gate.astype(F32))


def sink_softmax(s, sink_b):
    sink_b = sink_b.astype(F32)
    m = jnp.maximum(jnp.max(s, axis=-1, keepdims=True), sink_b)
    p = jnp.exp(s - m)
    return p / (jnp.sum(p, axis=-1, keepdims=True) + jnp.exp(sink_b - m))


def swa_prompt(q, k, v, valid, sink):
    b_, lp = q.shape[:2]
    nb = lp // WINDOW
    qb = q.reshape(b_, nb, WINDOW, SW_KV_HEADS, SW_GROUP, SW_HEAD_DIM)
    kb = k.reshape(b_, nb, WINDOW, SW_KV_HEADS, SW_HEAD_DIM)
    vb = v.reshape(b_, nb, WINDOW, SW_KV_HEADS, SW_HEAD_DIM)

    def prev(t):
        return jnp.pad(t, ((0, 0), (1, 0)) + ((0, 0),) * (t.ndim - 2))[:, :-1]

    kk = jnp.concatenate([prev(kb), kb], axis=2)
    vv = jnp.concatenate([prev(vb), vb], axis=2)
    valid_b = valid.reshape(nb, WINDOW)
    kvalid = jnp.concatenate([jnp.pad(valid_b, ((1, 0), (0, 0)))[:-1], valid_b], axis=1)
    qi = jnp.arange(WINDOW)[:, None] + WINDOW
    si = jnp.arange(2 * WINDOW)[None, :]
    band = (si <= qi) & (si >= qi - WINDOW)
    mask = band[None] & kvalid[:, None, :]
    s = jnp.einsum('bnqkgd,bnskd->bnkgqs', qb, kk, preferred_element_type=F32) * SW_SCALE
    s = jnp.where(mask[None, :, None, None], s, -jnp.inf)
    p = sink_softmax(s, sink.reshape(SW_KV_HEADS, SW_GROUP)[:, :, None, None])
    o = jnp.einsum('bnkgqs,bnskd->bnqkgd', p, vv.astype(F32))
    return o.reshape(b_, lp, SW_HEADS * SW_HEAD_DIM)


def swa_sample(q, k_new, v_new, ck, cv, sink):
    bd, s_ = q.shape[:2]
    wb = ck.shape[1]
    kk = jnp.concatenate([ck, k_new], axis=1)
    vv = jnp.concatenate([cv, v_new], axis=1)
    kpos = PAST_LEN - wb + jnp.arange(wb + s_)
    qpos = PAST_LEN + jnp.arange(s_)
    mask = (kpos[None, :] <= qpos[:, None]) & (kpos[None, :] >= qpos[:, None] - WINDOW)
    qg = q.reshape(bd, s_, SW_KV_HEADS, SW_GROUP, SW_HEAD_DIM)
    s = jnp.einsum('bqkgd,bskd->bkgqs', qg, kk, preferred_element_type=F32) * SW_SCALE
    s = jnp.where(mask[None, None, None], s, -jnp.inf)
    p = sink_softmax(s, sink.reshape(SW_KV_HEADS, SW_GROUP)[:, :, None, None])
    o = jnp.einsum('bkgqs,bskd->bqkgd', p, vv.astype(F32)).reshape(bd, s_, SW_HEADS * SW_HEAD_DIM)
    return o, kk[:, -wb:], vv[:, -wb:]


def mixer_prompt(h, lb, w_in, gnorm, sink, w_out):
    b_, l_ = h.shape[:2]
    pad = CHUNK - N_META
    lp = l_ + pad
    proj = jnp.einsum('bld,de->ble', h, w_in)
    proj = jnp.pad(proj, ((0, 0), (pad, 0), (0, 0)))
    valid = jnp.arange(lp) >= pad
    pos = jnp.arange(lp) - pad
    hq, hf, hi, hg, aq, ak, av = split_proj(proj)
    q, f, k, v = hgrn_features(hq, hf, hi, lb)
    vmask = valid[None, :, None, None]
    k = jnp.where(vmask, k, 0.0)
    logf = jnp.where(vmask, jnp.log(f), 0.0)
    o, s_fin = hgrn_chunked(q, logf, k, v)
    o_hg = hgrn_readout(o[:, pad:], hg[:, pad:], gnorm)
    aq = rotary(aq.reshape(b_, lp, SW_HEADS, SW_HEAD_DIM), pos)
    ak = rotary(ak.reshape(b_, lp, SW_KV_HEADS, SW_HEAD_DIM), pos)
    av = av.reshape(b_, lp, SW_KV_HEADS, SW_HEAD_DIM)
    o_sw = swa_prompt(aq, ak, av, valid, sink)[:, pad:]
    mix = jnp.concatenate([o_hg.astype(h.dtype), o_sw.astype(h.dtype)], axis=-1)
    y = jnp.einsum('ble,ed->bld', mix, w_out)
    wb = min(WINDOW, PAST_LEN)
    return y, s_fin, ak[:, -wb:], av[:, -wb:]


def mixer_sample(h, s0, ck, cv, lb, w_in, gnorm, sink, w_out):
    bd, s_ = h.shape[:2]
    proj = jnp.einsum('bld,de->ble', h, w_in)
    hq, hf, hi, hg, aq, ak, av = split_proj(proj)
    q, f, k, v = hgrn_features(hq, hf, hi, lb)
    tr = lambda t: jnp.swapaxes(t, 0, 1)
    s_new, o = lax.scan(hgrn_token_step, s0.astype(F32), (tr(q), tr(f), tr(k), tr(v)))
    o_hg = hgrn_readout(tr(o), hg, gnorm)
    pos = PAST_LEN + jnp.arange(s_)
    aq = rotary(aq.reshape(bd, s_, SW_HEADS, SW_HEAD_DIM), pos)
    ak = rotary(ak.reshape(bd, s_, SW_KV_HEADS, SW_HEAD_DIM), pos)
    av = av.reshape(bd, s_, SW_KV_HEADS, SW_HEAD_DIM)
    o_sw, k_buf, v_buf = swa_sample(aq, ak, av, ck.astype(ak.dtype), cv.astype(av.dtype), sink)
    mix = jnp.concatenate([o_hg.astype(h.dtype), o_sw.astype(h.dtype)], axis=-1)
    y = jnp.einsum('ble,ed->bld', mix, w_out)
    return y, s_new.astype(s0.dtype), k_buf.astype(ck.dtype), v_buf.astype(cv.dtype)


def peer_block(xb, wq, keys, u, v):
    t_ = xb.shape[0]
    qh = jnp.einsum('td,de->te', xb, wq, preferred_element_type=F32).reshape(t_, PEER_HEADS, 2, D_KEY)
    sc = jnp.einsum('thpd,hpnd->thpn', qh, keys.astype(F32))
    top_s, top_i = lax.top_k(sc, PEER_TOPK)
    cand = top_s[:, :, 0, :, None] + top_s[:, :, 1, None, :]
    best_s, best_c = lax.top_k(cand.reshape(t_, PEER_HEADS, PEER_TOPK * PEER_TOPK), PEER_TOPK)
    idx1 = jnp.take_along_axis(top_i[:, :, 0], best_c // PEER_TOPK, axis=-1)
    idx2 = jnp.take_along_axis(top_i[:, :, 1], best_c % PEER_TOPK, axis=-1)
    expert = idx1 * N_KEYS + idx2
    gate = jax.nn.softmax(best_s, axis=-1)
    u_sel = u[expert]
    act = jax.nn.gelu(jnp.einsum('td,thkd->thk', xb, u_sel, preferred_element_type=F32), approximate=False)
    v_sel = v[expert]
    return jnp.einsum('thk,thkd->td', gate * act, v_sel.astype(F32))


def peer(h, wq, keys, u, v):
    shp = h.shape
    xt = h.reshape(-1, shp[-1])
    n = xt.shape[0]
    nblk = -(-n // PEER_BLOCK)
    xt = jnp.pad(xt, ((0, nblk * PEER_BLOCK - n), (0, 0))).reshape(nblk, PEER_BLOCK, shp[-1])
    out = lax.map(lambda xb: peer_block(xb, wq, keys, u, v), xt)
    return out.reshape(-1, shp[-1])[:n].reshape(shp).astype(h.dtype)


def setup_inputs(seed: int = 0) -> dict:
    key = jax.random.key(seed)
    ks = jax.random.split(key, 20)
    wb = min(WINDOW, PAST_LEN)

    def nrm(k, shape, scale):
        return jax.random.normal(k, shape, F32) * scale

    return {
        "x_prompt": nrm(ks[0], (BATCH, SEQ, D_MODEL), 1.0),
        "x_sample": nrm(ks[1], (DEC_BATCH, DEC_SEQ, D_MODEL), 1.0),
        "state_hgrn": nrm(ks[2], (DEPTH, DEC_BATCH, HG_HEADS, HG_DK, HG_DV), 0.5),
        "cache_k": nrm(ks[3], (DEPTH, DEC_BATCH, wb, SW_KV_HEADS, SW_HEAD_DIM), 1.0),
        "cache_v": nrm(ks[4], (DEPTH, DEC_BATCH, wb, SW_KV_HEADS, SW_HEAD_DIM), 1.0),
        "meta_tokens": nrm(ks[5], (N_META, D_MODEL), 1.0),
        "lb_logits": nrm(ks[6], (DEPTH + 1, HG_QK), 0.5),
        "norm_mix": 1.0 + nrm(ks[7], (DEPTH, D_MODEL), 0.02),
        "w_in": nrm(ks[8], (DEPTH, D_MODEL, IN_WIDTH), D_MODEL ** -0.5),
        "hgrn_norm": 1.0 + nrm(ks[9], (DEPTH, HG_HEADS, HG_DV), 0.02),
        "attn_sink": nrm(ks[10], (DEPTH, SW_HEADS), 0.5),
        "w_out": nrm(ks[11], (DEPTH, MIX_WIDTH, D_MODEL), MIX_WIDTH ** -0.5),
        "norm_ffn": 1.0 + nrm(ks[12], (DEPTH, D_MODEL), 0.02),
        "peer_wq": nrm(ks[13], (DEPTH, D_MODEL, PEER_HEADS * 2 * D_KEY), D_MODEL ** -0.5),
        "peer_keys": nrm(ks[14], (DEPTH, PEER_HEADS, 2, N_KEYS, D_KEY), D_KEY ** -0.5),
        "peer_u": nrm(ks[15], (DEPTH, N_EXPERTS, D_MODEL), D_MODEL ** -0.5),
        "peer_v": nrm(ks[16], (DEPTH, N_EXPERTS, D_MODEL), PEER_HEADS ** -0.5),
        "norm_final": 1.0 + nrm(ks[17], (D_MODEL,), 0.02),
    }


def reference(x_prompt, x_sample, state_hgrn, cache_k, cache_v, meta_tokens, lb_logits, norm_mix, w_in,
              hgrn_norm, attn_sink, w_out, norm_ffn, peer_wq, peer_keys, peer_u, peer_v, norm_final):
    lower = jnp.cumsum(jax.nn.softmax(lb_logits.astype(F32), axis=0), axis=0)
    meta = jnp.broadcast_to(meta_tokens[None].astype(x_prompt.dtype), (x_prompt.shape[0], N_META, D_MODEL))
    hp = jnp.concatenate([meta, x_prompt], axis=1)
    hs = x_sample
    sp_list, kp_list, vp_list, ss_list, ks_list, vs_list = [], [], [], [], [], []
    for layer in range(DEPTH):
        a_p, s_p, k_p, v_p = mixer_prompt(rmsnorm(hp, norm_mix[layer]), lower[layer], w_in[layer],
                                          hgrn_norm[layer], attn_sink[layer], w_out[layer])
        a_s, s_s, k_s, v_s = mixer_sample(rmsnorm(hs, norm_mix[layer]), state_hgrn[layer], cache_k[layer],
                                          cache_v[layer], lower[layer], w_in[layer], hgrn_norm[layer],
                                          attn_sink[layer], w_out[layer])
        hp = hp + a_p
        hs = hs + a_s
        hp = hp + peer(rmsnorm(hp, norm_ffn[layer]), peer_wq[layer], peer_keys[layer], peer_u[layer], peer_v[layer])
        hs = hs + peer(rmsnorm(hs, norm_ffn[layer]), peer_wq[layer], peer_keys[layer], peer_u[layer], peer_v[layer])
        sp_list.append(s_p.astype(state_hgrn.dtype))
        kp_list.append(k_p.astype(cache_k.dtype))
        vp_list.append(v_p.astype(cache_v.dtype))
        ss_list.append(s_s)
        ks_list.append(k_s)
        vs_list.append(v_s)
    y_prompt = rmsnorm(hp[:, N_META:], norm_final)
    y_sample = rmsnorm(hs, norm_final)
    new_state_hgrn_prompt = jnp.stack(sp_list)
    new_cache_k_prompt = jnp.stack(kp_list)
    new_cache_v_prompt = jnp.stack(vp_list)
    new_state_hgrn_sample = jnp.stack(ss_list)
    new_cache_k_sample = jnp.stack(ks_list)
    new_cache_v_sample = jnp.stack(vs_list)
    return (y_prompt, y_sample, new_state_hgrn_prompt, new_cache_k_prompt, new_cache_v_prompt,
            new_state_hgrn_sample, new_cache_k_sample, new_cache_v_sample)
```

```python
import functools
import math

import jax
import jax.numpy as jnp
import numpy as np
from jax import lax
from jax.experimental import pallas as pl
from jax.experimental.pallas import tpu as pltpu

F32 = jnp.float32
BF16 = jnp.bfloat16

D_MODEL = 1024
N_META = 16
EPS = 1e-6
CHUNK = 128
PAD = CHUNK - N_META
HG_HEADS = 4
HG_DK = 128
HG_DV = 128
HG_QK = HG_HEADS * HG_DK
HG_WIDTH = HG_HEADS * HG_DV
SW_HEADS = 8
SW_KV = 2
SW_HD = 64
SW_GROUP = SW_HEADS // SW_KV
WINDOW = 128
SW_SCALE = SW_HD ** -0.5
ROPE_THETA = 10000.0
PAST_LEN = 8192
PEER_HEADS = 8
N_KEYS = 128
PEER_TOPK = 16
D_KEY = 128
N_EXPERTS = N_KEYS * N_KEYS

LANES = 128
SUBLANES = 8
VMEM_LIMIT = 56 * 1024 * 1024

AQ_OFF = 4 * HG_QK
AK_OFF = AQ_OFF + SW_HEADS * LANES
AV_OFF = AK_OFF + SW_KV * SW_HD
IN_EXP = AV_OFF + SW_KV * SW_HD
MIX_EXP = HG_WIDTH + SW_HEADS * LANES

LEVELS = (64, 32, 16, 8, 4, 2, 1)


def _sigmoid(x):
    return 1.0 / (1.0 + jnp.exp(-x))


def _rmsnorm(x, g):
    return x * lax.rsqrt(jnp.mean(x * x, axis=-1, keepdims=True) + EPS) * g


def _split3(x):
    x1 = x.astype(BF16)
    r1 = x - x1.astype(F32)
    x2 = r1.astype(BF16)
    r2 = r1 - x2.astype(F32)
    return x1, x2, r2.astype(BF16)


def _dot(a, b):
    return jnp.dot(a, b, preferred_element_type=F32)


def _dot_nt(a, b):
    return lax.dot_general(a, b, (((1,), (1,)), ((), ())), preferred_element_type=F32)


def _dot_tn(a, b):
    return lax.dot_general(a, b, (((0,), (0,)), ((), ())), preferred_element_type=F32)


def _lower_bound(lb_logits_ref):
    l = lb_logits_ref[...]
    m = jnp.max(l, axis=0, keepdims=True)
    e = jnp.exp(l - m)
    return e[0:1, :] / jnp.sum(e, axis=0, keepdims=True)


def _rotary(x, cos, sin_signed):
    n = x.shape[1] // LANES
    lane = lax.broadcasted_iota(jnp.int32, x.shape, 1)
    first_half = (lane & (SW_HD - 1)) < (SW_HD // 2)
    swapped = jnp.where(first_half, pltpu.roll(x, x.shape[1] - SW_HD // 2, 1), pltpu.roll(x, SW_HD // 2, 1))
    cosn = jnp.concatenate([cos] * n, axis=1) if n > 1 else cos
    sinn = jnp.concatenate([sin_signed] * n, axis=1) if n > 1 else sin_signed
    return x * cosn + swapped * sinn


def _hgrn_features(proj, lb, valid):
    hq = proj[:, 0:HG_QK]
    hf = proj[:, HG_QK:2 * HG_QK]
    q = hq * _sigmoid(hq)
    f = lb + (1.0 - lb) * _sigmoid(hf)
    if valid is None:
        return q, f, 1.0 - f
    logf = jnp.where(valid, jnp.log(f), 0.0)
    k = jnp.where(valid, 1.0 - f, 0.0)
    return q, logf, k


def _hgrn_readout(o, gate_raw, gnorm):
    o = o * lax.rsqrt(jnp.mean(o * o, axis=-1, keepdims=True) + EPS) * gnorm
    return o * (gate_raw * _sigmoid(gate_raw))


def _mixer_prompt_kernel(xm_ref, x_ref, cos_ref, sin_ref, nm_ref, lbl_ref, win_ref, cst_ref, blk_ref, gn_ref,
                         sink_ref, wout_ref,
                         h1_ref, st_ref, ck_ref, cv_ref,
                         st_scr, kprev_scr, vprev_scr):
    c = pl.program_id(1)
    last = pl.num_programs(1) - 1

    @pl.when(c == 0)
    def _init():
        st_scr[...] = jnp.zeros_like(st_scr)
        kprev_scr[...] = jnp.zeros_like(kprev_scr)
        vprev_scr[...] = jnp.zeros_like(vprev_scr)

    x = jnp.where(c == 0, xm_ref[...], x_ref[...])
    row = lax.broadcasted_iota(jnp.int32, (CHUNK, 1), 0)
    valid = jnp.logical_or(c > 0, row >= PAD)
    xn = _rmsnorm(x, nm_ref[...]).astype(BF16)
    proj = _dot(xn, win_ref[...])
    lb = _lower_bound(lbl_ref)
    q, logf, k = _hgrn_features(proj, lb, valid)
    v = proj[:, 2 * HG_QK:3 * HG_QK]
    gate_raw = proj[:, 3 * HG_QK:4 * HG_QK]

    mix_parts = []
    for h in range(HG_HEADS):
        sl = slice(h * HG_DK, (h + 1) * HG_DK)
        qh, kh, vh = q[:, sl], k[:, sl], v[:, sl]
        g1, g2, g3 = _split3(logf[:, sl])
        br = _dot(cst_ref[...], jnp.concatenate([g1, g2, g3], axis=1))
        br = br[:, 0:HG_DK] + br[:, HG_DK:2 * HG_DK] + br[:, 2 * HG_DK:3 * HG_DK]
        b = br[0:CHUNK]
        a = None
        for i, p in enumerate(LEVELS):
            bnd = br[(i + 1) * CHUNK:(i + 2) * CHUNK]
            upper = ((row >> int(math.log2(p))) & 1) == 1
            e = jnp.exp(jnp.where(upper, b - bnd, bnd - b))
            qt = jnp.where(upper, qh * e, 0.0).astype(BF16)
            kt = jnp.where(upper, 0.0, kh * e).astype(BF16)
            m = _dot_nt(qt, kt)
            a = m if i == 0 else a + m * blk_ref[i]
        st = st_scr[h]
        o = (_dot(a.astype(BF16), vh.astype(BF16))
             + jnp.sum(qh * kh, axis=1, keepdims=True) * vh
             + _dot_nt((qh * jnp.exp(b)).astype(BF16), st.astype(BF16)))
        b_last = b[CHUNK - 1:CHUNK, :]
        khat = kh * jnp.exp(b_last - b)
        st_scr[h] = st * jnp.exp(b_last) + _dot_tn(vh.astype(BF16), khat.astype(BF16))
        mix_parts.append(_hgrn_readout(o, gate_raw[:, sl], gn_ref[h:h + 1, :]).astype(BF16))

    cos = cos_ref[...]
    sin = sin_ref[...]
    qr = _rotary(proj[:, AQ_OFF:AK_OFF], cos, sin)
    kr = _rotary(proj[:, AK_OFF:AV_OFF], cos, sin)
    av = proj[:, AV_OFF:IN_EXP]
    kk = jnp.concatenate([kprev_scr[...], kr], axis=0).astype(BF16)
    vv = jnp.concatenate([vprev_scr[...], av], axis=0).astype(BF16)
    t = lax.broadcasted_iota(jnp.int32, (CHUNK, 2 * CHUNK), 0)
    krel = lax.broadcasted_iota(jnp.int32, (CHUNK, 2 * CHUNK), 1) - CHUNK
    lo = jnp.where(c == 0, PAD, jnp.where(c == 1, PAD - CHUNK, -CHUNK))
    ninf = jnp.float32(-jnp.inf)
    bias = jnp.where(krel <= t, jnp.where(krel >= t - WINDOW, jnp.where(krel >= lo, 0.0, ninf), ninf), ninf)
    lane = lax.broadcasted_iota(jnp.int32, (CHUNK, LANES), 1)
    for h in range(SW_HEADS):
        j = h // SW_GROUP
        sc = _dot_nt(qr[:, h * LANES:(h + 1) * LANES].astype(BF16), kk) * SW_SCALE + bias
        sink = sink_ref[h]
        m = jnp.maximum(jnp.max(sc, axis=1, keepdims=True), sink)
        p = jnp.exp(sc - m)
        den = jnp.sum(p, axis=1, keepdims=True) + jnp.exp(sink - m)
        oh = _dot((p / den).astype(BF16), vv)
        in_kv = jnp.logical_and(lane >= j * SW_HD, lane < (j + 1) * SW_HD)
        mix_parts.append(jnp.where(in_kv, oh, 0.0).astype(BF16))

    mix = jnp.concatenate(mix_parts, axis=1)
    h1_ref[...] = x + _dot(mix, wout_ref[...])
    kprev_scr[...] = kr
    vprev_scr[...] = av

    @pl.when(c == last)
    def _fin():
        for h in range(HG_HEADS):
            st_ref[h] = st_scr[h].T
        ck_ref[...] = kr
        cv_ref[...] = av


def _level_constants():
    t = np.arange(CHUNK)
    rows = [(t[None, :] <= t[:, None])]
    blocks = []
    for p in LEVELS:
        bnd = (t // (2 * p)) * (2 * p) + p - 1
        rows.append(t[None, :] <= bnd[:, None])
        blocks.append((t[:, None] // (2 * p)) == (t[None, :] // (2 * p)))
    cst = np.concatenate(rows, axis=0).astype(np.float32)
    blk = np.stack(blocks, axis=0).astype(np.float32)
    return jnp.asarray(cst, BF16), jnp.asarray(blk, F32)


def _expand_w_in(w_in):
    d = w_in.shape[0]
    head = w_in[:, 4 * HG_QK:4 * HG_QK + SW_HEADS * SW_HD].reshape(d, SW_HEADS, SW_HD)
    aq = jnp.zeros((d, SW_HEADS, SW_KV, SW_HD), w_in.dtype)
    for h in range(SW_HEADS):
        aq = aq.at[:, h, h // SW_GROUP, :].set(head[:, h, :])
    rest = w_in[:, 4 * HG_QK + SW_HEADS * SW_HD:]
    return jnp.concatenate([w_in[:, :4 * HG_QK], aq.reshape(d, SW_HEADS * LANES), rest], axis=1).astype(BF16)


def _expand_w_out(w_out):
    d = w_out.shape[1]
    sw = w_out[HG_WIDTH:].reshape(SW_HEADS, SW_HD, d)
    ex = jnp.zeros((SW_HEADS, SW_KV, SW_HD, d), w_out.dtype)
    for h in range(SW_HEADS):
        ex = ex.at[h, h // SW_GROUP].set(sw[h])
    return jnp.concatenate([w_out[:HG_WIDTH], ex.reshape(SW_HEADS * LANES, d)], axis=0).astype(BF16)


def _rope_tables(pos):
    half = SW_HD // 2
    inv = ROPE_THETA ** (-jnp.arange(half, dtype=F32) * 2.0 / SW_HD)
    ang = pos.astype(F32)[:, None] * inv[None, :]
    cos = jnp.cos(ang)
    sin = jnp.sin(ang)
    cos_t = jnp.concatenate([cos, cos] * (LANES // SW_HD), axis=1)
    sin_t = jnp.concatenate([-sin, sin] * (LANES // SW_HD), axis=1)
    return cos_t, sin_t


def _const_spec(shape):
    nd = len(shape)
    return pl.BlockSpec(shape, lambda *_: (0,) * nd)


def _mixer_prompt(x_prompt, meta_tokens, lb_logits, norm_mix, w_in_e, hgrn_norm, attn_sink, w_out_e):
    bsz, seq, d = x_prompt.shape
    nch = seq // CHUNK + 1
    xm = jnp.concatenate([jnp.zeros((PAD, d), x_prompt.dtype), meta_tokens.astype(x_prompt.dtype)], axis=0)
    cos_t, sin_t = _rope_tables(jnp.arange(nch * CHUNK) - PAD)
    cst, blk = _level_constants()
    xmap = lambda b, c: (b, jnp.maximum(c - 1, 0), 0)
    out_shape = (
        jax.ShapeDtypeStruct((bsz, seq, d), F32),
        jax.ShapeDtypeStruct((bsz, HG_HEADS, HG_DK, HG_DV), F32),
        jax.ShapeDtypeStruct((bsz, CHUNK, SW_KV * SW_HD), F32),
        jax.ShapeDtypeStruct((bsz, CHUNK, SW_KV * SW_HD), F32),
    )
    return pl.pallas_call(
        _mixer_prompt_kernel,
        grid=(bsz, nch),
        in_specs=[
            _const_spec((CHUNK, d)),
            pl.BlockSpec((None, CHUNK, d), xmap),
            pl.BlockSpec((CHUNK, LANES), lambda b, c: (c, 0)),
            pl.BlockSpec((CHUNK, LANES), lambda b, c: (c, 0)),
            _const_spec((1, d)),
            _const_spec(lb_logits.shape),
            _const_spec(w_in_e.shape),
            _const_spec(cst.shape),
            _const_spec(blk.shape),
            _const_spec(hgrn_norm.shape),
            pl.BlockSpec(memory_space=pltpu.SMEM),
            _const_spec(w_out_e.shape),
        ],
        out_specs=(
            pl.BlockSpec((None, CHUNK, d), xmap),
            pl.BlockSpec((None, HG_HEADS, HG_DK, HG_DV), lambda b, c: (b, 0, 0, 0)),
            pl.BlockSpec((None, CHUNK, SW_KV * SW_HD), lambda b, c: (b, 0, 0)),
            pl.BlockSpec((None, CHUNK, SW_KV * SW_HD), lambda b, c: (b, 0, 0)),
        ),
        out_shape=out_shape,
        scratch_shapes=[
            pltpu.VMEM((HG_HEADS, HG_DV, HG_DK), F32),
            pltpu.VMEM((CHUNK, SW_KV * SW_HD), F32),
            pltpu.VMEM((CHUNK, SW_KV * SW_HD), F32),
        ],
        compiler_params=pltpu.CompilerParams(
            dimension_semantics=("arbitrary", "arbitrary"), vmem_limit_bytes=VMEM_LIMIT),
        name="mixer_prompt",
    )(xm, x_prompt, cos_t, sin_t, norm_mix.reshape(1, d), lb_logits, w_in_e, cst, blk, hgrn_norm, attn_sink, w_out_e)


SAMPLE_BB = 8


def _mixer_sample_kernel(x_ref, st_ref, ck_ref, cv_ref, cos_ref, sin_ref, nm_ref, lbl_ref, win_ref, gn_ref, sink_ref,
                         wout_ref,
                         h1_ref, nst_ref, nck_ref, ncv_ref,
                         q_scr, f_scr, k_scr, v_scr, qa_scr, ka_scr, va_scr, o_scr, mix_scr):
    i = pl.program_id(0)
    last = pl.num_programs(0) - 1
    nb = x_ref.shape[0]

    @pl.when(i == 0)
    def _project():
        xn = _rmsnorm(x_ref[...], nm_ref[...]).astype(BF16)
        proj = _dot(xn, win_ref[...])
        q, f, k = _hgrn_features(proj, _lower_bound(lbl_ref), None)
        q_scr[...] = q
        f_scr[...] = f
        k_scr[...] = k
        v_scr[...] = proj[:, 2 * HG_QK:3 * HG_QK]
        cos = jnp.broadcast_to(cos_ref[...], (nb, LANES))
        sin = jnp.broadcast_to(sin_ref[...], (nb, LANES))
        qr = _rotary(proj[:, AQ_OFF:AK_OFF], cos, sin)
        for h in range(SW_HEADS):
            qa_scr[pl.ds(h, nb, stride=SW_HEADS), :] = qr[:, h * LANES:(h + 1) * LANES]
        ka_scr[...] = _rotary(proj[:, AK_OFF:AV_OFF], cos, sin)
        va_scr[...] = proj[:, AV_OFF:IN_EXP]

    row8 = lax.broadcasted_iota(jnp.int32, (SAMPLE_BB, LANES), 0)
    lane8 = lax.broadcasted_iota(jnp.int32, (SAMPLE_BB, LANES), 1)
    in_kv = (lane8 >> int(math.log2(SW_HD))) == (row8 >> int(math.log2(SW_GROUP)))
    base = pl.multiple_of(i * SAMPLE_BB, SAMPLE_BB)
    rows = pl.ds(base, SAMPLE_BB)
    q8b, f8b, k8b, v8b = q_scr[rows, :], f_scr[rows, :], k_scr[rows, :], v_scr[rows, :]
    ka8, va8 = ka_scr[rows, :], va_scr[rows, :]
    o_rows = [jnp.zeros((SAMPLE_BB, HG_DV), F32) for _ in range(HG_HEADS)]
    att_rows = [jnp.zeros((SAMPLE_BB, LANES), F32) for _ in range(SW_HEADS)]
    for bi in range(SAMPLE_BB):
        for h in range(HG_HEADS):
            sl = slice(h * HG_DK, (h + 1) * HG_DK)
            fcol = jnp.broadcast_to(f8b[bi:bi + 1, sl], (HG_DK, HG_DK)).T
            kcol = jnp.broadcast_to(k8b[bi:bi + 1, sl], (HG_DK, HG_DK)).T
            s_new = fcol * st_ref[bi, h] + kcol * v8b[bi:bi + 1, sl]
            nst_ref[bi, h] = s_new
            o8 = _dot(q8b[:, sl].astype(BF16), s_new.astype(BF16))
            o_rows[h] = jnp.where(row8 == bi, o8, o_rows[h])
        q8 = qa_scr[pl.ds(pl.multiple_of((base + bi) * SW_HEADS, SW_HEADS), SW_HEADS), :]
        ck = ck_ref[bi]
        cv = cv_ref[bi]
        k_new = ka8[bi:bi + 1, :]
        v_new = va8[bi:bi + 1, :]
        sc = _dot_nt(q8.astype(BF16), ck.astype(BF16)) * SW_SCALE
        sc_new = jnp.sum(q8 * k_new, axis=1, keepdims=True) * SW_SCALE
        sink = sink_ref[...]
        m = jnp.maximum(jnp.maximum(jnp.max(sc, axis=1, keepdims=True), sc_new), sink)
        p = jnp.exp(sc - m)
        p_new = jnp.exp(sc_new - m)
        den = jnp.sum(p, axis=1, keepdims=True) + p_new + jnp.exp(sink - m)
        o_att = (_dot(p.astype(BF16), cv.astype(BF16)) + p_new * v_new) / den
        o_att = jnp.where(in_kv, o_att, 0.0)
        for h in range(SW_HEADS):
            att_rows[h] = jnp.where(row8 == bi, jnp.broadcast_to(o_att[h:h + 1, :], (SAMPLE_BB, LANES)), att_rows[h])
        nck_ref[bi, 0:WINDOW - 1, :] = ck[1:WINDOW, :]
        nck_ref[bi, WINDOW - 1:WINDOW, :] = k_new
        ncv_ref[bi, 0:WINDOW - 1, :] = cv[1:WINDOW, :]
        ncv_ref[bi, WINDOW - 1:WINDOW, :] = v_new
    for h in range(HG_HEADS):
        o_scr[rows, h * HG_DV:(h + 1) * HG_DV] = o_rows[h]
    for h in range(SW_HEADS):
        mix_scr[rows, HG_WIDTH + h * LANES:HG_WIDTH + (h + 1) * LANES] = att_rows[h]

    @pl.when(i == last)
    def _finish():
        xn = _rmsnorm(x_ref[...], nm_ref[...]).astype(BF16)
        gate_raw = _dot(xn, win_ref[:, 3 * HG_QK:4 * HG_QK])
        for h in range(HG_HEADS):
            sl = slice(h * HG_DV, (h + 1) * HG_DV)
            mix_scr[:, sl] = _hgrn_readout(o_scr[:, sl], gate_raw[:, sl], gn_ref[h:h + 1, :])
        h1_ref[...] = x_ref[...] + _dot(mix_scr[...].astype(BF16), wout_ref[...])


def _mixer_sample(x_sample, state, cache_k, cache_v, lb_logits, norm_mix, w_in_e, hgrn_norm, attn_sink, w_out_e):
    nb, d = x_sample.shape
    assert nb % SAMPLE_BB == 0 and cache_k.shape[1] == WINDOW and PAST_LEN >= WINDOW
    cos_t, sin_t = _rope_tables(jnp.full((1,), PAST_LEN, jnp.int32))
    kvw = SW_KV * SW_HD
    out_shape = (
        jax.ShapeDtypeStruct((nb, d), F32),
        jax.ShapeDtypeStruct(state.shape, state.dtype),
        jax.ShapeDtypeStruct(cache_k.shape, cache_k.dtype),
        jax.ShapeDtypeStruct(cache_v.shape, cache_v.dtype),
    )
    st_spec = pl.BlockSpec((SAMPLE_BB, HG_HEADS, HG_DK, HG_DV), lambda i: (i, 0, 0, 0))
    kv_spec = pl.BlockSpec((SAMPLE_BB, WINDOW, kvw), lambda i: (i, 0, 0))
    return pl.pallas_call(
        _mixer_sample_kernel,
        grid=(nb // SAMPLE_BB,),
        in_specs=[
            _const_spec((nb, d)), st_spec, kv_spec, kv_spec,
            _const_spec((1, LANES)), _const_spec((1, LANES)), _const_spec((1, d)), _const_spec(lb_logits.shape),
            _const_spec(w_in_e.shape), _const_spec(hgrn_norm.shape), _const_spec((SW_HEADS, 1)),
            _const_spec(w_out_e.shape),
        ],
        out_specs=(_const_spec((nb, d)), st_spec, kv_spec, kv_spec),
        out_shape=out_shape,
        scratch_shapes=[
            pltpu.VMEM((nb, HG_QK), F32), pltpu.VMEM((nb, HG_QK), F32), pltpu.VMEM((nb, HG_QK), F32),
            pltpu.VMEM((nb, HG_WIDTH), F32),
            pltpu.VMEM((SW_HEADS * nb, LANES), F32), pltpu.VMEM((nb, kvw), F32), pltpu.VMEM((nb, kvw), F32),
            pltpu.VMEM((nb, HG_WIDTH), F32), pltpu.VMEM((nb, MIX_EXP), F32),
        ],
        compiler_params=pltpu.CompilerParams(dimension_semantics=("arbitrary",), vmem_limit_bytes=VMEM_LIMIT),
        name="mixer_sample",
    )(x_sample, state, cache_k, cache_v, cos_t, sin_t, norm_mix.reshape(1, d), lb_logits, w_in_e, hgrn_norm,
      attn_sink.reshape(SW_HEADS, 1), w_out_e)


N_HP = PEER_HEADS * 2
SLOTS = PEER_HEADS * PEER_TOPK
BIG = 1.0e9


def _candidate_tables():
    flat, bias = [], []
    def add(k1, k2, ok=True):
        flat.append(float(k1 * PEER_TOPK + k2) if ok else BIG)
        bias.append(0.0 if ok else -np.inf)
    for k2 in range(PEER_TOPK):
        add(0, k2)
    for k1 in range(1, SUBLANES):
        for k2 in range(SUBLANES):
            add(k1, k2, (k1 + 1) * (k2 + 1) <= PEER_TOPK)
    for k1 in range(SUBLANES, PEER_TOPK):
        add(k1, 0)
    return np.asarray(flat, np.float32), np.asarray(bias, np.float32)


def _topk_rounds(x, ids, n):
    vals, sel = [], []
    for _ in range(n):
        m = jnp.max(x, axis=0, keepdims=True)
        pick = jnp.min(jnp.where(x == m, ids, BIG), axis=0, keepdims=True)
        vals.append(m)
        sel.append(pick)
        x = jnp.where(ids == pick, -jnp.inf, x)
    return jnp.concatenate(vals, axis=0), jnp.concatenate(sel, axis=0)


def _peer_route_kernel(h_ref, nf_ref, wqh_ref, wql_ref, kh_ref, kl_ref, cflat_ref, cbias_ref,
                       xn_ref, i1_ref, i2_ref, g_ref,
                       sc_scr, ts_scr, ti_scr, o1_scr, o2_scr, og_scr):
    td = h_ref.shape[0]
    xn = _rmsnorm(h_ref[...], nf_ref[...])
    x_hi = xn.astype(BF16)
    x_lo = (xn - x_hi.astype(F32)).astype(BF16)
    xn_ref[...] = x_hi
    qh = _dot(x_hi, wqh_ref[...]) + _dot(x_hi, wql_ref[...]) + _dot(x_lo, wqh_ref[...])
    for hp in range(N_HP):
        q = qh[:, hp * D_KEY:(hp + 1) * D_KEY]
        q_hi = q.astype(BF16)
        q_lo = (q - q_hi.astype(F32)).astype(BF16)
        k_hi, k_lo = kh_ref[hp], kl_ref[hp]
        sc_scr[hp] = _dot_nt(k_hi, q_hi) + _dot_nt(k_hi, q_lo) + _dot_nt(k_lo, q_hi)

    key_id = lax.broadcasted_iota(jnp.int32, (N_KEYS, td), 0).astype(F32)

    def level1(hp, carry):
        vals, ids = _topk_rounds(sc_scr[hp], key_id, PEER_TOPK)
        ts_scr[hp] = vals
        ti_scr[hp] = ids
        return carry

    lax.fori_loop(0, N_HP, level1, 0)

    def level2(h, carry):
        s1, s2 = ts_scr[2 * h], ts_scr[2 * h + 1]
        i1, i2 = ti_scr[2 * h], ti_scr[2 * h + 1]
        blocks = [s1[0:1] + s2]
        for k1 in range(1, SUBLANES):
            blocks.append(s1[k1:k1 + 1] + s2[0:SUBLANES])
        blocks.append(s1[SUBLANES:PEER_TOPK] + s2[0:1])
        cand = jnp.concatenate(blocks, axis=0) + cbias_ref[...]
        best, flat = _topk_rounds(cand, cflat_ref[...], PEER_TOPK)
        k1sel = jnp.floor(flat * (1.0 / PEER_TOPK))
        k2sel = flat - k1sel * PEER_TOPK
        e1 = jnp.zeros_like(flat)
        e2 = jnp.zeros_like(flat)
        for kk in range(PEER_TOPK):
            e1 = jnp.where(k1sel == kk, i1[kk:kk + 1], e1)
            e2 = jnp.where(k2sel == kk, i2[kk:kk + 1], e2)
        p = jnp.exp(best - best[0:1])
        gate = p / jnp.sum(p, axis=0, keepdims=True)
        rows = pl.ds(pl.multiple_of(h * PEER_TOPK, PEER_TOPK), PEER_TOPK)
        o1_scr[rows, :] = e1
        o2_scr[rows, :] = e2
        og_scr[rows, :] = gate
        return carry

    lax.fori_loop(0, PEER_HEADS, level2, 0)
    i1_ref[...] = o1_scr[...].T
    i2_ref[...] = o2_scr[...].T
    g_ref[...] = og_scr[...].T


def _peer_route(h1, norm_ffn, wq_hi, wq_lo, keys_hi, keys_lo, td):
    n, d = h1.shape
    cflat, cbias = _candidate_tables()
    ncand = cflat.shape[0]
    cflat = jnp.asarray(np.broadcast_to(cflat[:, None], (ncand, td)))
    cbias = jnp.asarray(np.broadcast_to(cbias[:, None], (ncand, td)))
    tok = lambda i: (i, 0)
    out_shape = (jax.ShapeDtypeStruct((n, d), BF16),) + (jax.ShapeDtypeStruct((n, SLOTS), F32),) * 3
    return pl.pallas_call(
        _peer_route_kernel,
        grid=(n // td,),
        in_specs=[
            pl.BlockSpec((td, d), tok), _const_spec((1, d)), _const_spec(wq_hi.shape), _const_spec(wq_lo.shape),
            _const_spec(keys_hi.shape), _const_spec(keys_lo.shape), _const_spec(cflat.shape), _const_spec(cbias.shape),
        ],
        out_specs=(pl.BlockSpec((td, d), tok),) + (pl.BlockSpec((td, SLOTS), tok),) * 3,
        out_shape=out_shape,
        scratch_shapes=[
            pltpu.VMEM((N_HP, N_KEYS, td), F32), pltpu.VMEM((N_HP, PEER_TOPK, td), F32),
            pltpu.VMEM((N_HP, PEER_TOPK, td), F32),
            pltpu.VMEM((SLOTS, td), F32), pltpu.VMEM((SLOTS, td), F32), pltpu.VMEM((SLOTS, td), F32),
        ],
        compiler_params=pltpu.CompilerParams(dimension_semantics=("arbitrary",), vmem_limit_bytes=VMEM_LIMIT),
        name="peer_route",
    )(h1, norm_ffn.reshape(1, d), wq_hi, wq_lo, keys_hi, keys_lo, cflat, cbias)


E_STEP = 1024
W_STRIDE = N_KEYS + SUBLANES
SQRT_HALF = 0.7071067811865476


def _peer_expert_kernel(xn_ref, h_ref, i1_ref, i2_ref, g_ref, ut_ref, v_ref, nfin_ref, y_ref, w_scr, acc_scr):
    j = pl.program_id(1)
    tb = xn_ref.shape[0]

    @pl.when(j == 0)
    def _build_gate_matrix():
        acc_scr[...] = jnp.zeros_like(acc_scr)
        key_id = lax.broadcasted_iota(jnp.int32, (N_KEYS, SLOTS), 0).astype(F32)

        def group(gi, carry):
            rows = pl.ds(pl.multiple_of(gi * SUBLANES, SUBLANES), SUBLANES)
            i1, i2, gt = i1_ref[rows, :], i2_ref[rows, :], g_ref[rows, :]
            for r in range(SUBLANES):
                p = jnp.where(key_id == i1[r:r + 1, :], gt[r:r + 1, :], 0.0).astype(BF16)
                qt = jnp.where(key_id == i2[r:r + 1, :], 1.0, 0.0).astype(BF16)
                start = pl.multiple_of(gi * (SUBLANES * W_STRIDE), SUBLANES) + r * W_STRIDE
                w_scr[pl.ds(start, N_KEYS), :] = _dot_nt(p, qt)
            return carry

        lax.fori_loop(0, tb // SUBLANES, group, 0)

    hid = _dot(xn_ref[...], ut_ref[...])
    c0 = pl.multiple_of(j * (E_STEP // N_KEYS), SUBLANES)
    span = (tb - 1) * W_STRIDE + SUBLANES
    w_view = w_scr.at[pl.ds(c0, span), :]
    wc = jnp.concatenate([w_view[pl.ds(cc, tb, stride=W_STRIDE), :] for cc in range(E_STEP // N_KEYS)], axis=1)
    act = 0.5 * hid * (1.0 + lax.erf(hid * SQRT_HALF))
    acc_scr[...] += _dot((act * wc).astype(BF16), v_ref[...])

    @pl.when(j == pl.num_programs(1) - 1)
    def _finish():
        y_ref[...] = _rmsnorm(h_ref[...] + acc_scr[...], nfin_ref[...])


def _peer_experts(xn, h1, i1, i2, gate, u_t, v_b, norm_final, tb):
    n, d = h1.shape
    tok = lambda i, j: (i, 0)
    return pl.pallas_call(
        _peer_expert_kernel,
        grid=(n // tb, N_EXPERTS // E_STEP),
        in_specs=[
            pl.BlockSpec((tb, d), tok), pl.BlockSpec((tb, d), tok),
            pl.BlockSpec((tb, SLOTS), tok), pl.BlockSpec((tb, SLOTS), tok), pl.BlockSpec((tb, SLOTS), tok),
            pl.BlockSpec((d, E_STEP), lambda i, j: (0, j)), pl.BlockSpec((E_STEP, d), lambda i, j: (j, 0)),
            _const_spec((1, d)),
        ],
        out_specs=pl.BlockSpec((tb, d), tok),
        out_shape=jax.ShapeDtypeStruct((n, d), F32),
        scratch_shapes=[pltpu.VMEM((tb * W_STRIDE, N_KEYS), F32), pltpu.VMEM((tb, d), F32)],
        compiler_params=pltpu.CompilerParams(
            dimension_semantics=("arbitrary", "arbitrary"), vmem_limit_bytes=VMEM_LIMIT),
        name="peer_experts",
    )(xn, h1, i1, i2, gate, u_t, v_b, norm_final.reshape(1, d))


def _peer_block(h1, norm_ffn, wq_hi, wq_lo, keys_hi, keys_lo, u_t, v_b, norm_final, td, tb):
    xn, i1, i2, gate = _peer_route(h1, norm_ffn, wq_hi, wq_lo, keys_hi, keys_lo, td)
    return _peer_experts(xn, h1, i1, i2, gate, u_t, v_b, norm_final, tb)


PROMPT_TD = 256
PROMPT_TB = 256


def _hi_lo(w):
    hi = w.astype(BF16)
    return hi, (w - hi.astype(F32)).astype(BF16)


def kernel(x_prompt, x_sample, state_hgrn, cache_k, cache_v, meta_tokens, lb_logits, norm_mix, w_in, hgrn_norm, attn_sink, w_out, norm_ffn, peer_wq, peer_keys, peer_u, peer_v, norm_final):
    assert w_in.shape[0] == 1 and lb_logits.shape[0] == 2, "single-layer trunk"
    bsz, seq, d = x_prompt.shape
    nb = x_sample.shape[0]
    kvw = SW_KV * SW_HD
    w_in_e = _expand_w_in(w_in[0])
    w_out_e = _expand_w_out(w_out[0])
    wq_hi, wq_lo = _hi_lo(peer_wq[0])
    keys_hi, keys_lo = _hi_lo(peer_keys[0].reshape(N_HP, N_KEYS, D_KEY))
    u_t = peer_u[0].T.astype(BF16)
    v_b = peer_v[0].astype(BF16)

    h1p, st_p, ck_p, cv_p = _mixer_prompt(x_prompt, meta_tokens, lb_logits, norm_mix[0], w_in_e, hgrn_norm[0],
                                          attn_sink[0], w_out_e)
    h1s, st_s, ck_s, cv_s = _mixer_sample(x_sample[:, 0], state_hgrn[0], cache_k[0].reshape(nb, WINDOW, kvw),
                                          cache_v[0].reshape(nb, WINDOW, kvw), lb_logits, norm_mix[0], w_in_e,
                                          hgrn_norm[0], attn_sink[0], w_out_e)
    peer = functools.partial(_peer_block, norm_ffn=norm_ffn[0], wq_hi=wq_hi, wq_lo=wq_lo, keys_hi=keys_hi,
                             keys_lo=keys_lo, u_t=u_t, v_b=v_b, norm_final=norm_final)
    y_p = peer(h1p.reshape(bsz * seq, d), td=PROMPT_TD, tb=PROMPT_TB).reshape(bsz, seq, d)
    y_s = peer(h1s, td=nb, tb=nb).reshape(nb, 1, d)
    kv_shape = (1, -1, WINDOW, SW_KV, SW_HD)
    return (y_p, y_s, st_p[None], ck_p.reshape(kv_shape), cv_p.reshape(kv_shape),
            st_s[None], ck_s.reshape(kv_shape), cv_s.reshape(kv_shape))
```
